```python
import math
import jax, jax.numpy as jnp
from jax import lax
import numpy as np

D_MODEL = 1024
BATCH = 2
SEQ = 8192
DEPTH = 2

CHUNK = 64
D_MIX = D_MODEL
D_ATTN = D_MIX // 2
D_LRU = D_MIX - D_ATTN
N_ATTN_HEADS = 8
HEAD_DIM = D_ATTN // N_ATTN_HEADS
N_LRU_BLOCKS = 8
LRU_BLOCK = D_LRU // N_LRU_BLOCKS
LRU_CONV_W = 4
LRU_C = 8.0
LEFT_CHUNKS = 8
BAND = (LEFT_CHUNKS + 1) * CHUNK
MAX_REL = 128
N_REL = (CHUNK - 1) + MAX_REL + 1
D_FF = 2816
FFN_CONV_W = 3
D_IN = 3 * D_ATTN + 2 * D_LRU
ALPHA = (2 * DEPTH) ** 0.25
BETA = (8 * DEPTH) ** -0.25
LN_EPS = 1e-5
NEG_INF = -1e30

kernel_name = "hymba_rglru_chunkattn_convffn_deepnorm"


def layer_norm(x, g, b):
    xf = x.astype(jnp.float32)
    mu = jnp.mean(xf, axis=-1, keepdims=True)
    var = jnp.mean(jnp.square(xf - mu), axis=-1, keepdims=True)
    y = (xf - mu) * lax.rsqrt(var + LN_EPS)
    return (y * g.astype(jnp.float32) + b.astype(jnp.float32)).astype(x.dtype)


def causal_dwconv(x, w, b):
    k_w = w.shape[0]
    s = x.shape[1]
    xp = jnp.pad(x, ((0, 0), (k_w - 1, 0), (0, 0)))
    y = b
    for k in range(k_w):
        y = y + w[k] * xp[:, k:k + s]
    return y


def rg_lru(x, w_a, b_a, w_x, b_x, lam):
    bsz, s, _ = x.shape
    xb = x.reshape(bsz, s, N_LRU_BLOCKS, LRU_BLOCK)
    r = jax.nn.sigmoid(jnp.einsum('bshi,hij->bshj', xb, w_a).reshape(bsz, s, D_LRU) + b_a)
    i = jax.nn.sigmoid(jnp.einsum('bshi,hij->bshj', xb, w_x).reshape(bsz, s, D_LRU) + b_x)
    log_a = -LRU_C * r.astype(jnp.float32) * jax.nn.softplus(-lam.astype(jnp.float32))
    a = jnp.exp(log_a)
    mult = jnp.sqrt(-jnp.expm1(2.0 * log_a))
    u = mult * (i * x).astype(jnp.float32)

    def combine(left, right):
        a1, b1 = left
        a2, b2 = right
        return a1 * a2, a2 * b1 + b2

    _, h = lax.associative_scan(combine, (a, u), axis=1)
    return h.astype(x.dtype)


def chunk_band_attention(q, k, v, rel_table):
    bsz, s, _ = q.shape
    nc = s // CHUNK
    shp = (bsz, nc, CHUNK, N_ATTN_HEADS, HEAD_DIM)
    q = q.reshape(shp)
    k = k.reshape(shp)
    v = v.reshape(shp)
    pad = ((0, 0), (LEFT_CHUNKS, 0), (0, 0), (0, 0), (0, 0))
    kp = jnp.pad(k, pad)
    vp = jnp.pad(v, pad)
    kb = jnp.concatenate([kp[:, j:j + nc] for j in range(LEFT_CHUNKS + 1)], axis=2)
    vb = jnp.concatenate([vp[:, j:j + nc] for j in range(LEFT_CHUNKS + 1)], axis=2)
    scores = jnp.einsum('bcqhd,bckhd->bhcqk', q, kb).astype(jnp.float32) * (HEAD_DIM ** -0.5)
    qi = jnp.arange(CHUNK)[:, None]
    km = jnp.arange(BAND)[None, :]
    dist = LEFT_CHUNKS * CHUNK + qi - km
    idx = jnp.clip(dist, -(CHUNK - 1), MAX_REL) + (CHUNK - 1)
    bias = rel_table[:, idx].astype(jnp.float32)
    cidx = jnp.arange(nc)[:, None]
    band_chunk = (jnp.arange(BAND) // CHUNK)[None, :]
    valid = (cidx + band_chunk - LEFT_CHUNKS) >= 0
    scores = scores + bias[None, :, None]
    scores = jnp.where(valid[None, None, :, None, :], scores, NEG_INF)
    p = jax.nn.softmax(scores, axis=-1).astype(v.dtype)
    o = jnp.einsum('bhcqk,bckhd->bcqhd', p, vb)
    return o.reshape(bsz, s, D_ATTN)


def hybrid_mixer(x, w_in, b_in, rel_table, lconv_w, lconv_b, w_a, b_a, w_x, b_x, lam, w_out, b_out):
    z = x @ w_in + b_in
    q, k, v, xl, gl = jnp.split(
        z, [D_ATTN, 2 * D_ATTN, 3 * D_ATTN, 3 * D_ATTN + D_LRU], axis=-1)
    y_attn = chunk_band_attention(q, k, v, rel_table)
    xl = causal_dwconv(xl, lconv_w, lconv_b)
    y_lru = rg_lru(xl, w_a, b_a, w_x, b_x, lam) * jax.nn.gelu(gl)
    y = jnp.concatenate([y_attn, y_lru], axis=-1)
    return y @ w_out + b_out


def conv_gated_ffn(x, w_up, b_up, fconv_w, fconv_b, w_down, b_down):
    u = x @ w_up + b_up
    u = causal_dwconv(u, fconv_w, fconv_b)
    val, gate = jnp.split(u, 2, axis=-1)
    return (jax.nn.gelu(gate) * val) @ w_down + b_down


def setup_inputs(seed: int = 0) -> dict:
    key = jax.random.key(seed)
    ks = jax.random.split(key, 32)
    f32 = jnp.float32
    nrm = lambda k, shape, scale: jax.random.normal(k, shape, f32) * scale
    x = jax.random.normal(ks[0], (BATCH, SEQ, D_MODEL), f32)
    ln_in_g = 1.0 + nrm(ks[1], (D_MODEL,), 0.02)
    ln_in_b = nrm(ks[2], (D_MODEL,), 0.02)
    w_in = nrm(ks[3], (DEPTH, D_MODEL, D_IN), D_MODEL ** -0.5)
    col_scale = jnp.concatenate([jnp.ones((2 * D_ATTN,), f32), jnp.full((D_ATTN,), BETA, f32),
                                 jnp.ones((2 * D_LRU,), f32)])
    w_in = w_in * col_scale
    b_in = nrm(ks[4], (DEPTH, D_IN), 0.02)
    rel_bias = nrm(ks[5], (DEPTH, N_ATTN_HEADS, N_REL), 0.5)
    lru_conv_w = nrm(ks[6], (DEPTH, LRU_CONV_W, D_LRU), LRU_CONV_W ** -0.5)
    lru_conv_b = nrm(ks[7], (DEPTH, D_LRU), 0.02)
    lru_w_a = nrm(ks[8], (DEPTH, N_LRU_BLOCKS, LRU_BLOCK, LRU_BLOCK), LRU_BLOCK ** -0.5)
    lru_b_a = nrm(ks[9], (DEPTH, D_LRU), 0.02)
    lru_w_x = nrm(ks[10], (DEPTH, N_LRU_BLOCKS, LRU_BLOCK, LRU_BLOCK), LRU_BLOCK ** -0.5)
    lru_b_x = nrm(ks[11], (DEPTH, D_LRU), 0.02)
    a_c = jax.random.uniform(ks[12], (DEPTH, D_LRU), f32, 0.9, 0.999)
    a_base = a_c ** (1.0 / LRU_C)
    lru_lambda = jnp.log(a_base) - jnp.log1p(-a_base)
    w_out = nrm(ks[13], (DEPTH, D_MIX, D_MODEL), BETA * D_MIX ** -0.5)
    b_out = nrm(ks[14], (DEPTH, D_MODEL), 0.02)
    ln1_g = 1.0 + nrm(ks[15], (DEPTH, D_MODEL), 0.02)
    ln1_b = nrm(ks[16], (DEPTH, D_MODEL), 0.02)
    ffn_w_up = nrm(ks[17], (DEPTH, D_MODEL, 2 * D_FF), D_MODEL ** -0.5)
    ffn_b_up = nrm(ks[18], (DEPTH, 2 * D_FF), 0.02)
    ffn_conv_w = nrm(ks[19], (DEPTH, FFN_CONV_W, 2 * D_FF), FFN_CONV_W ** -0.5)
    ffn_conv_b = nrm(ks[20], (DEPTH, 2 * D_FF), 0.02)
    ffn_w_down = nrm(ks[21], (DEPTH, D_FF, D_MODEL), BETA * D_FF ** -0.5)
    ffn_b_down = nrm(ks[22], (DEPTH, D_MODEL), 0.02)
    ln2_g = 1.0 + nrm(ks[23], (DEPTH, D_MODEL), 0.02)
    ln2_b = nrm(ks[24], (DEPTH, D_MODEL), 0.02)
    return {"x": x, "ln_in_g": ln_in_g, "ln_in_b": ln_in_b, "w_in": w_in, "b_in": b_in,
            "rel_bias": rel_bias, "lru_conv_w": lru_conv_w, "lru_conv_b": lru_conv_b,
            "lru_w_a": lru_w_a, "lru_b_a": lru_b_a, "lru_w_x": lru_w_x, "lru_b_x": lru_b_x,
            "lru_lambda": lru_lambda, "w_out": w_out, "b_out": b_out,
            "ln1_g": ln1_g, "ln1_b": ln1_b, "ffn_w_up": ffn_w_up, "ffn_b_up": ffn_b_up,
            "ffn_conv_w": ffn_conv_w, "ffn_conv_b": ffn_conv_b, "ffn_w_down": ffn_w_down,
            "ffn_b_down": ffn_b_down, "ln2_g": ln2_g, "ln2_b": ln2_b}


def reference(x, ln_in_g, ln_in_b, w_in, b_in, rel_bias, lru_conv_w, lru_conv_b,
              lru_w_a, lru_b_a, lru_w_x, lru_b_x, lru_lambda, w_out, b_out,
              ln1_g, ln1_b, ffn_w_up, ffn_b_up, ffn_conv_w, ffn_conv_b, ffn_w_down,
              ffn_b_down, ln2_g, ln2_b):
    h = layer_norm(x, ln_in_g, ln_in_b)
    for l in range(DEPTH):
        mix = hybrid_mixer(h, w_in[l], b_in[l], rel_bias[l], lru_conv_w[l], lru_conv_b[l],
                           lru_w_a[l], lru_b_a[l], lru_w_x[l], lru_b_x[l], lru_lambda[l],
                           w_out[l], b_out[l])
        h = layer_norm(ALPHA * h + mix, ln1_g[l], ln1_b[l])
        ffn = conv_gated_ffn(h, ffn_w_up[l], ffn_b_up[l], ffn_conv_w[l], ffn_conv_b[l],
                             ffn_w_down[l], ffn_b_down[l])
        h = layer_norm(ALPHA * h + ffn, ln2_g[l], ln2_b[l])
    return h
```

```python
import functools
import math

import jax
import jax.numpy as jnp
from jax import lax
from jax.experimental import pallas as pl
from jax.experimental.pallas import tpu as pltpu

D_MODEL = 1024
DEPTH = 2
CHUNK = 64
D_ATTN = 512
D_LRU = 512
N_HEADS = 8
HEAD_DIM = 64
N_LRU_BLOCKS = 8
LRU_BLOCK = 64
LRU_CONV_W = 4
LRU_C = 8.0
LEFT_CHUNKS = 8
HIST = LEFT_CHUNKS * CHUNK
MAX_REL = 128
D_FF = 2816
FFN_CONV_W = 3
ALPHA = (2 * DEPTH) ** 0.25
LN_EPS = 1e-5
NEG_INF = -1e30

SUBLANES_F32 = 8
SUBLANES_BF16 = 16
MIX_TILE = 512
Q_BLOCK = 256
KEY_WIN = Q_BLOCK + HIST
FFN_TILE = 512
FFN_HALO = SUBLANES_BF16
FF_CHUNK = 256
VMEM_LIMIT_BYTES = 56 * 1024 * 1024


def _layer_norm(x, g, b):
    mu = jnp.mean(x, axis=-1, keepdims=True)
    xc = x - mu
    var = jnp.mean(xc * xc, axis=-1, keepdims=True)
    return xc * lax.rsqrt(var + LN_EPS) * g + b


def _gelu(x):
    return 0.5 * x * (1.0 + jnp.tanh(math.sqrt(2.0 / math.pi) * (x + 0.044715 * (x * x * x))))


def _sigmoid(x):
    return 1.0 / (1.0 + jnp.exp(-x))


def _softplus(x):
    return jnp.maximum(x, 0.0) + jnp.log1p(jnp.exp(-jnp.abs(x)))


def _expm1(x):
    e = jnp.exp(x)
    near_zero = (e - 1.0) * x / jnp.log(e)
    return jnp.where(e == 1.0, x, jnp.where(x < -1.0, e - 1.0, near_zero))


def _bdot(a, b):
    return jnp.dot(a, b, preferred_element_type=jnp.float32)


def _linear_scan(a, u):
    rows = a.shape[0]
    row = lax.broadcasted_iota(jnp.int32, a.shape, 0)
    d = 1
    while d < rows:
        if d < SUBLANES_F32:
            keep = row >= d
            a_prev = jnp.where(keep, pltpu.roll(a, d, 0), 1.0)
            u_prev = jnp.where(keep, pltpu.roll(u, d, 0), 0.0)
            u = a * u_prev + u
            a = a * a_prev
        else:
            u = jnp.concatenate([u[:d], a[d:] * u[:-d] + u[d:]], axis=0)
            a = jnp.concatenate([a[:d], a[d:] * a[:-d]], axis=0)
        d *= 2
    return a, u


def _mixer_kernel(x_ref, lng_ref, lnb_ref, w_in_ref, b_in_ref, bias_ref, cw_ref, cb_ref,
                  wa_ref, ba_ref, wx_ref, bx_ref, lam_ref, w_out_ref, b_out_ref, g_ref, b_ref,
                  o_ref, k_buf, v_buf, xl_buf, h_buf, y_buf, *, pre_ln):
    t = pl.program_id(1)
    tile = MIX_TILE

    @pl.when(t == 0)
    def _():
        k_buf[0:HIST, :] = jnp.zeros((HIST, D_ATTN), jnp.bfloat16)
        v_buf[0:HIST, :] = jnp.zeros((HIST, D_ATTN), jnp.bfloat16)
        xl_buf[0:SUBLANES_F32, :] = jnp.zeros((SUBLANES_F32, D_LRU), jnp.float32)
        h_buf[...] = jnp.zeros((SUBLANES_F32, D_LRU), jnp.float32)

    x = x_ref[0]
    if pre_ln:
        x = _layer_norm(x, lng_ref[...], lnb_ref[...])
    xb = x.astype(jnp.bfloat16)

    def proj(lo, width):
        return _bdot(xb, w_in_ref[:, lo:lo + width]) + b_in_ref[:, lo:lo + width]

    q = (proj(0, D_ATTN) * (HEAD_DIM ** -0.5)).astype(jnp.bfloat16)
    k_buf[HIST:HIST + tile, :] = proj(D_ATTN, D_ATTN).astype(jnp.bfloat16)
    v_buf[HIST:HIST + tile, :] = proj(2 * D_ATTN, D_ATTN).astype(jnp.bfloat16)

    col = lax.broadcasted_iota(jnp.int32, (1, KEY_WIN), 1)
    for qb in range(tile // Q_BLOCK):
        r0 = qb * Q_BLOCK
        first_valid = HIST - r0 - t * tile
        colmask = jnp.where(col >= first_valid, 0.0, NEG_INF)
        for h in range(N_HEADS):
            c0 = h * HEAD_DIM
            qh = q[r0:r0 + Q_BLOCK, c0:c0 + HEAD_DIM]
            kh = k_buf[r0:r0 + KEY_WIN, c0:c0 + HEAD_DIM]
            vh = v_buf[r0:r0 + KEY_WIN, c0:c0 + HEAD_DIM]
            s = lax.dot_general(qh, kh, (((1,), (1,)), ((), ())),
                                preferred_element_type=jnp.float32)
            s = s + bias_ref[h] + colmask
            m = jnp.max(s, axis=-1, keepdims=True)
            p = jnp.exp(s - m)
            l = jnp.sum(p, axis=-1, keepdims=True)
            o = _bdot(p.astype(jnp.bfloat16), vh) / l
            y_buf[r0:r0 + Q_BLOCK, c0:c0 + HEAD_DIM] = o.astype(jnp.bfloat16)

    k_buf[0:HIST, :] = k_buf[tile:tile + HIST, :]
    v_buf[0:HIST, :] = v_buf[tile:tile + HIST, :]

    xl_buf[SUBLANES_F32:SUBLANES_F32 + tile, :] = proj(3 * D_ATTN, D_LRU)
    xc = cb_ref[...]
    for kk in range(LRU_CONV_W):
        off = SUBLANES_F32 - (LRU_CONV_W - 1) + kk
        xc = xc + cw_ref[kk:kk + 1, :] * xl_buf[off:off + tile, :]
    xl_buf[0:SUBLANES_F32, :] = xl_buf[tile:tile + SUBLANES_F32, :]

    xcb = xc.astype(jnp.bfloat16)
    r = _sigmoid(_bdot(xcb, wa_ref[...]) + ba_ref[...])
    i = _sigmoid(_bdot(xcb, wx_ref[...]) + bx_ref[...])
    log_a = (-LRU_C) * r * _softplus(-lam_ref[...])
    a = jnp.exp(log_a)
    u = jnp.sqrt(-_expm1(2.0 * log_a)) * (i * xc)
    a_cum, hs = _linear_scan(a, u)
    hs = hs + a_cum * h_buf[SUBLANES_F32 - 1:SUBLANES_F32, :]
    h_buf[...] = hs[tile - SUBLANES_F32:tile, :]
    gl = proj(3 * D_ATTN + D_LRU, D_LRU)
    y_buf[:, D_ATTN:D_ATTN + D_LRU] = (hs * _gelu(gl)).astype(jnp.bfloat16)

    mix = _bdot(y_buf[...], w_out_ref[...]) + b_out_ref[...]
    o_ref[0] = _layer_norm(ALPHA * x + mix, g_ref[...], b_ref[...])


def _ffn_kernel(x_ref, w_up_ref, b_up_ref, fw_ref, fb_ref, w_dn_ref, b_dn_ref, g_ref, b_ref,
                o_ref, xe_buf, u_buf, acc_buf):
    t = pl.program_id(1)
    tile = FFN_TILE
    halo = FFN_HALO

    @pl.when(t == 0)
    def _():
        xe_buf[0:halo, :] = jnp.zeros((halo, D_MODEL), jnp.bfloat16)

    x = x_ref[0]
    xe_buf[halo:halo + tile, :] = x.astype(jnp.bfloat16)
    xe = xe_buf[...]
    not_first = t > 0

    for j in range(D_FF // FF_CHUNK):
        cols = (j * FF_CHUNK, D_FF + j * FF_CHUNK)
        halves = []
        for part, c0 in enumerate(cols):
            u = _bdot(xe, w_up_ref[:, c0:c0 + FF_CHUNK]) + b_up_ref[:, c0:c0 + FF_CHUNK]
            lo = part * FF_CHUNK
            u_buf[0:halo, lo:lo + FF_CHUNK] = jnp.where(not_first, u[0:halo], 0.0)
            u_buf[halo:halo + tile, lo:lo + FF_CHUNK] = u[halo:halo + tile]
            c = fb_ref[:, c0:c0 + FF_CHUNK]
            for kk in range(FFN_CONV_W):
                off = halo - (FFN_CONV_W - 1) + kk
                c = c + fw_ref[kk:kk + 1, c0:c0 + FF_CHUNK] * u_buf[off:off + tile, lo:lo + FF_CHUNK]
            halves.append(c)
        val, gate = halves
        gated = (_gelu(gate) * val).astype(jnp.bfloat16)
        part_out = _bdot(gated, w_dn_ref[j * FF_CHUNK:(j + 1) * FF_CHUNK, :])
        if j == 0:
            acc_buf[...] = part_out
        else:
            acc_buf[...] += part_out

    xe_buf[0:halo, :] = xe_buf[tile:tile + halo, :]
    ffn = acc_buf[...] + b_dn_ref[...]
    o_ref[0] = _layer_norm(ALPHA * x + ffn, g_ref[...], b_ref[...])


def _const_spec(shape):
    return pl.BlockSpec(shape, lambda b, t: (0,) * len(shape), pipeline_mode=pl.Buffered(1))


def _seq_spec(tile):
    return pl.BlockSpec((1, tile, D_MODEL), lambda b, t: (b, t, 0))


def _compiler_params():
    return pltpu.CompilerParams(dimension_semantics=("arbitrary", "arbitrary"),
                                vmem_limit_bytes=VMEM_LIMIT_BYTES)


def _mixer(x, lng, lnb, w_in, b_in, biasmask, cw, cb, wa, ba, wx, bx, lam, w_out, b_out, g, b,
           *, pre_ln):
    bsz, seq, _ = x.shape
    consts = (lng, lnb, w_in, b_in, biasmask, cw, cb, wa, ba, wx, bx, lam, w_out, b_out, g, b)
    return pl.pallas_call(
        functools.partial(_mixer_kernel, pre_ln=pre_ln),
        grid=(bsz, seq // MIX_TILE),
        in_specs=[_seq_spec(MIX_TILE)] + [_const_spec(c.shape) for c in consts],
        out_specs=_seq_spec(MIX_TILE),
        out_shape=jax.ShapeDtypeStruct(x.shape, jnp.float32),
        scratch_shapes=[
            pltpu.VMEM((HIST + MIX_TILE, D_ATTN), jnp.bfloat16),
            pltpu.VMEM((HIST + MIX_TILE, D_ATTN), jnp.bfloat16),
            pltpu.VMEM((SUBLANES_F32 + MIX_TILE, D_LRU), jnp.float32),
            pltpu.VMEM((SUBLANES_F32, D_LRU), jnp.float32),
            pltpu.VMEM((MIX_TILE, D_MODEL), jnp.bfloat16),
        ],
        compiler_params=_compiler_params(),
        name="mixer_pre_ln" if pre_ln else "mixer",
    )(x, *consts)


def _ffn(x, w_up, b_up, fw, fb, w_dn, b_dn, g, b):
    bsz, seq, _ = x.shape
    consts = (w_up, b_up, fw, fb, w_dn, b_dn, g, b)
    return pl.pallas_call(
        _ffn_kernel,
        grid=(bsz, seq // FFN_TILE),
        in_specs=[_seq_spec(FFN_TILE)] + [_const_spec(c.shape) for c in consts],
        out_specs=_seq_spec(FFN_TILE),
        out_shape=jax.ShapeDtypeStruct(x.shape, jnp.float32),
        scratch_shapes=[
            pltpu.VMEM((FFN_HALO + FFN_TILE, D_MODEL), jnp.bfloat16),
            pltpu.VMEM((FFN_HALO + FFN_TILE, 2 * FF_CHUNK), jnp.float32),
            pltpu.VMEM((FFN_TILE, D_MODEL), jnp.float32),
        ],
        compiler_params=_compiler_params(),
        name="conv_ffn",
    )(x, *consts)


def _bias_mask(rel_table):
    qi = jnp.arange(Q_BLOCK)[:, None]
    kj = jnp.arange(KEY_WIN)[None, :]
    dist = HIST + qi - kj
    idx = jnp.clip(dist, -(CHUNK - 1), MAX_REL) + (CHUNK - 1)
    qc, kc = qi // CHUNK, kj // CHUNK
    in_band = (kc >= qc) & (kc <= qc + LEFT_CHUNKS)
    return jnp.where(in_band[None], rel_table[:, idx], NEG_INF).astype(jnp.float32)


def _block_diag(w):
    nb, n, _ = w.shape
    eye = jnp.eye(nb, dtype=w.dtype)
    return jnp.einsum("hij,hg->higj", w, eye).reshape(nb * n, nb * n)


def kernel(x, ln_in_g, ln_in_b, w_in, b_in, rel_bias, lru_conv_w, lru_conv_b, lru_w_a, lru_b_a, lru_w_x, lru_b_x, lru_lambda, w_out, b_out, ln1_g, ln1_b, ffn_w_up, ffn_b_up, ffn_conv_w, ffn_conv_b, ffn_w_down, ffn_b_down, ln2_g, ln2_b):
    bf16 = jnp.bfloat16
    row = lambda v: v.reshape(1, -1)
    h = x
    for l in range(DEPTH):
        h = _mixer(h, row(ln_in_g), row(ln_in_b), w_in[l].astype(bf16), row(b_in[l]),
                   _bias_mask(rel_bias[l]), lru_conv_w[l], row(lru_conv_b[l]),
                   _block_diag(lru_w_a[l]).astype(bf16), row(lru_b_a[l]),
                   _block_diag(lru_w_x[l]).astype(bf16), row(lru_b_x[l]), row(lru_lambda[l]),
                   w_out[l].astype(bf16), row(b_out[l]), row(ln1_g[l]), row(ln1_b[l]),
                   pre_ln=(l == 0))
        h = _ffn(h, ffn_w_up[l].astype(bf16), row(ffn_b_up[l]), ffn_conv_w[l],
                 row(ffn_conv_b[l]), ffn_w_down[l].astype(bf16), row(ffn_b_down[l]),
                 row(ln2_g[l]), row(ln2_b[l]))
    return h
```

```python
import functools
import math

import jax
import jax.numpy as jnp
from jax import lax
from jax.experimental import pallas as pl
from jax.experimental.pallas import tpu as pltpu

D_MODEL = 1024
DEPTH = 2
CHUNK = 64
D_ATTN = 512
D_LRU = 512
N_HEADS = 8
HEAD_DIM = 64
N_LRU_BLOCKS = 8
LRU_BLOCK = 64
LRU_CONV_W = 4
LRU_C = 8.0
LEFT_CHUNKS = 8
HIST = LEFT_CHUNKS * CHUNK
MAX_REL = 128
N_REL = (CHUNK - 1) + MAX_REL + 1
D_FF = 2816
FFN_CONV_W = 3
ALPHA = (2 * DEPTH) ** 0.25
LN_EPS = 1e-5
NEG_INF = -1e30

LANES = 128
SUBLANES_F32 = 8

MIX_TILE = 512
Q_BLOCK = 256
KEY_WIN = Q_BLOCK + HIST
TOE_LANES = 1024
assert TOE_LANES >= KEY_WIN + Q_BLOCK - 1
FFN_TILE = 512
FFN_SEG = FFN_TILE // SUBLANES_F32
FF_CHUNK = 256
N_SLABS = D_MODEL // LANES
VMEM_LIMIT_BYTES = 56 * 1024 * 1024

_GELU_K1 = -2.0 * math.sqrt(2.0 / math.pi) * math.log2(math.e)
_GELU_K3 = _GELU_K1 * 0.044715


def _layer_norm(x, g, b):
    mu = jnp.mean(x, axis=-1, keepdims=True)
    xc = x - mu
    var = jnp.mean(xc * xc, axis=-1, keepdims=True)
    return xc * lax.rsqrt(var + LN_EPS) * g + b


def _gelu(x):
    return 0.5 * x * (1.0 + jnp.tanh(math.sqrt(2.0 / math.pi) * (x + 0.044715 * (x * x * x))))


def _gelu_mul(g, v):
    e = jnp.exp2(g * (_GELU_K1 + _GELU_K3 * (g * g)))
    return (g * v) / (1.0 + e)


def _sigmoid(x):
    return 1.0 / (1.0 + jnp.exp(-x))


def _softplus(x):
    return jnp.maximum(x, 0.0) + jnp.log1p(jnp.exp(-jnp.abs(x)))


def _expm1(x):
    e = jnp.exp(x)
    near_zero = (e - 1.0) * x / jnp.log(e)
    return jnp.where(e == 1.0, x, jnp.where(x < -1.0, e - 1.0, near_zero))


def _bdot(a, b):
    return jnp.dot(a, b, preferred_element_type=jnp.float32)


def _linear_scan(a, u):
    rows = a.shape[0]
    row = lax.broadcasted_iota(jnp.int32, a.shape, 0)
    d = 1
    while d < rows:
        if d < SUBLANES_F32:
            keep = row >= d
            a_prev = jnp.where(keep, pltpu.roll(a, d, 0), 1.0)
            u_prev = jnp.where(keep, pltpu.roll(u, d, 0), 0.0)
            u = a * u_prev + u
            a = a * a_prev
        else:
            u = jnp.concatenate([u[:d], a[d:] * u[:-d] + u[d:]], axis=0)
            a = jnp.concatenate([a[:d], a[d:] * a[:-d]], axis=0)
        d *= 2
    return a, u


def _mixer_kernel(x_ref, lng_ref, lnb_ref, w_in_ref, b_in_ref, rel_ref, cw_ref, cb_ref,
                  wa_ref, ba_ref, wx_ref, bx_ref, lam_ref, w_out_ref, b_out_ref, g_ref, b_ref,
                  o_ref, bias_buf, k_buf, v_buf, xl_buf, h_buf, y_buf, *, pre_ln):
    t = pl.program_id(1)
    tile = MIX_TILE

    @pl.when((pl.program_id(0) == 0) & (t == 0))
    def _():
        qc = lax.broadcasted_iota(jnp.int32, (Q_BLOCK, KEY_WIN), 0) // CHUNK
        kc = lax.broadcasted_iota(jnp.int32, (Q_BLOCK, KEY_WIN), 1) // CHUNK
        in_band = (kc >= qc) & (kc <= qc + LEFT_CHUNKS)
        for h in range(N_HEADS):
            rows = jnp.broadcast_to(rel_ref[h:h + 1, :], (Q_BLOCK, TOE_LANES))
            toe = pltpu.roll(rows, 0, 1, stride=1, stride_axis=0)
            bias_buf[h] = jnp.where(in_band, toe[:, :KEY_WIN], NEG_INF)

    @pl.when(t == 0)
    def _():
        k_buf[0:HIST, :] = jnp.zeros((HIST, D_ATTN), jnp.bfloat16)
        v_buf[0:HIST, :] = jnp.zeros((HIST, D_ATTN), jnp.bfloat16)
        xl_buf[0:SUBLANES_F32, :] = jnp.zeros((SUBLANES_F32, D_LRU), jnp.float32)
        h_buf[...] = jnp.zeros((SUBLANES_F32, D_LRU), jnp.float32)

    x = x_ref[0]
    if pre_ln:
        x = _layer_norm(x, lng_ref[...], lnb_ref[...])
    xb = x.astype(jnp.bfloat16)

    def proj(lo, width):
        return _bdot(xb, w_in_ref[:, lo:lo + width]) + b_in_ref[:, lo:lo + width]

    q = (proj(0, D_ATTN) * (HEAD_DIM ** -0.5)).astype(jnp.bfloat16)
    k_buf[HIST:HIST + tile, :] = proj(D_ATTN, D_ATTN).astype(jnp.bfloat16)
    v_buf[HIST:HIST + tile, :] = proj(2 * D_ATTN, D_ATTN).astype(jnp.bfloat16)

    col = lax.broadcasted_iota(jnp.int32, (1, KEY_WIN), 1)
    for qb in range(tile // Q_BLOCK):
        r0 = qb * Q_BLOCK
        first_valid = HIST - r0 - t * tile
        colmask = jnp.where(col >= first_valid, 0.0, NEG_INF)
        for h in range(N_HEADS):
            c0 = h * HEAD_DIM
            qh = q[r0:r0 + Q_BLOCK, c0:c0 + HEAD_DIM]
            kh = k_buf[r0:r0 + KEY_WIN, c0:c0 + HEAD_DIM]
            vh = v_buf[r0:r0 + KEY_WIN, c0:c0 + HEAD_DIM]
            s = lax.dot_general(qh, kh, (((1,), (1,)), ((), ())),
                                preferred_element_type=jnp.float32)
            s = s + bias_buf[h] + colmask
            m = jnp.max(s, axis=-1, keepdims=True)
            p = jnp.exp(s - m)
            l = jnp.sum(p, axis=-1, keepdims=True)
            o = _bdot(p.astype(jnp.bfloat16), vh) / l
            y_buf[r0:r0 + Q_BLOCK, c0:c0 + HEAD_DIM] = o.astype(jnp.bfloat16)

    k_buf[0:HIST, :] = k_buf[tile:tile + HIST, :]
    v_buf[0:HIST, :] = v_buf[tile:tile + HIST, :]

    xl_buf[SUBLANES_F32:SUBLANES_F32 + tile, :] = proj(3 * D_ATTN, D_LRU)
    xc = cb_ref[...]
    for kk in range(LRU_CONV_W):
        off = SUBLANES_F32 - (LRU_CONV_W - 1) + kk
        xc = xc + cw_ref[kk:kk + 1, :] * xl_buf[off:off + tile, :]
    xl_buf[0:SUBLANES_F32, :] = xl_buf[tile:tile + SUBLANES_F32, :]

    xcb = xc.astype(jnp.bfloat16)
    r = _sigmoid(_bdot(xcb, wa_ref[...]) + ba_ref[...])
    i = _sigmoid(_bdot(xcb, wx_ref[...]) + bx_ref[...])
    log_a = (-LRU_C) * r * _softplus(-lam_ref[...])
    a = jnp.exp(log_a)
    u = jnp.sqrt(-_expm1(2.0 * log_a)) * (i * xc)
    a_cum, hs = _linear_scan(a, u)
    hs = hs + a_cum * h_buf[SUBLANES_F32 - 1:SUBLANES_F32, :]
    h_buf[...] = hs[tile - SUBLANES_F32:tile, :]
    gl = proj(3 * D_ATTN + D_LRU, D_LRU)
    y_buf[:, D_ATTN:D_ATTN + D_LRU] = (hs * _gelu(gl)).astype(jnp.bfloat16)

    mix = _bdot(y_buf[...], w_out_ref[...]) + b_out_ref[...]
    o_ref[0] = _layer_norm(ALPHA * x + mix, g_ref[...], b_ref[...])


def _ffn_kernel(x_ref, w_up_ref, b_up_ref, fw_ref, fb_ref, w_dn_ref, b_dn_ref, g_ref, b_ref,
                o_ref, xs_buf, tail_buf, gated_buf, ys_buf):
    t = pl.program_id(1)
    tile, seg, sub = FFN_TILE, FFN_SEG, SUBLANES_F32

    @pl.when(t == 0)
    def _():
        tail_buf[...] = jnp.zeros(tail_buf.shape, jnp.float32)

    for c in range(N_SLABS):
        for s in range(sub):
            xs_buf[c, pl.ds(s, seg, stride=sub), :] = (
                x_ref[0, s * seg:(s + 1) * seg, c * LANES:(c + 1) * LANES])

    def load_x():
        return jnp.concatenate([xs_buf[c] for c in range(N_SLABS)], axis=1)

    xb = load_x().astype(jnp.bfloat16)
    first_seg = lax.broadcasted_iota(jnp.int32, (sub, FF_CHUNK), 0) == 0

    def conv_chunk(c0):
        cols = slice(c0, c0 + FF_CHUNK)
        u = _bdot(xb, w_up_ref[:, cols]) + b_up_ref[:, cols]
        prev_tail = tail_buf[:, cols]
        tail_buf[:, cols] = u[tile - 2 * sub:tile]

        def one_segment_down(cur, prev):
            return jnp.where(first_seg, pltpu.roll(prev, 1, 0), pltpu.roll(cur, 1, 0))

        back2 = one_segment_down(u[tile - 2 * sub:tile - sub], prev_tail[0:sub])
        back1 = one_segment_down(u[tile - sub:tile], prev_tail[sub:2 * sub])
        u1 = jnp.concatenate([back1, u[:tile - sub]], axis=0)
        u2 = jnp.concatenate([back2, back1, u[:tile - 2 * sub]], axis=0)
        return (fb_ref[:, cols] + fw_ref[2:3, cols] * u + fw_ref[1:2, cols] * u1
                + fw_ref[0:1, cols] * u2)

    for j in range(D_FF // FF_CHUNK):
        val = conv_chunk(j * FF_CHUNK)
        gate = conv_chunk(D_FF + j * FF_CHUNK)
        gated_buf[:, j * FF_CHUNK:(j + 1) * FF_CHUNK] = _gelu_mul(gate, val).astype(jnp.bfloat16)

    ffn = _bdot(gated_buf[...], w_dn_ref[...]) + b_dn_ref[...]
    y = _layer_norm(ALPHA * load_x() + ffn, g_ref[...], b_ref[...])
    for c in range(N_SLABS):
        ys_buf[c] = y[:, c * LANES:(c + 1) * LANES]
    for c in range(N_SLABS):
        for s in range(sub):
            o_ref[0, s * seg:(s + 1) * seg, c * LANES:(c + 1) * LANES] = (
                ys_buf[c, pl.ds(s, seg, stride=sub), :])


def _const_spec(shape):
    return pl.BlockSpec(shape, lambda b, t: (0,) * len(shape), pipeline_mode=pl.Buffered(1))


def _seq_spec(tile):
    return pl.BlockSpec((1, tile, D_MODEL), lambda b, t: (b, t, 0))


def _compiler_params():
    return pltpu.CompilerParams(dimension_semantics=("arbitrary", "arbitrary"),
                                vmem_limit_bytes=VMEM_LIMIT_BYTES)


def _mixer(x, lng, lnb, w_in, b_in, rel_rows, cw, cb, wa, ba, wx, bx, lam, w_out, b_out, g, b,
           *, pre_ln):
    bsz, seq, _ = x.shape
    consts = (lng, lnb, w_in, b_in, rel_rows, cw, cb, wa, ba, wx, bx, lam, w_out, b_out, g, b)
    return pl.pallas_call(
        functools.partial(_mixer_kernel, pre_ln=pre_ln),
        grid=(bsz, seq // MIX_TILE),
        in_specs=[_seq_spec(MIX_TILE)] + [_const_spec(c.shape) for c in consts],
        out_specs=_seq_spec(MIX_TILE),
        out_shape=jax.ShapeDtypeStruct(x.shape, jnp.float32),
        scratch_shapes=[
            pltpu.VMEM((N_HEADS, Q_BLOCK, KEY_WIN), jnp.float32),
            pltpu.VMEM((HIST + MIX_TILE, D_ATTN), jnp.bfloat16),
            pltpu.VMEM((HIST + MIX_TILE, D_ATTN), jnp.bfloat16),
            pltpu.VMEM((SUBLANES_F32 + MIX_TILE, D_LRU), jnp.float32),
            pltpu.VMEM((SUBLANES_F32, D_LRU), jnp.float32),
            pltpu.VMEM((MIX_TILE, D_MODEL), jnp.bfloat16),
        ],
        compiler_params=_compiler_params(),
        name="mixer_pre_ln" if pre_ln else "mixer",
    )(x, *consts)


def _ffn(x, w_up, b_up, fw, fb, w_dn, b_dn, g, b):
    bsz, seq, _ = x.shape
    consts = (w_up, b_up, fw, fb, w_dn, b_dn, g, b)
    return pl.pallas_call(
        _ffn_kernel,
        grid=(bsz, seq // FFN_TILE),
        in_specs=[_seq_spec(FFN_TILE)] + [_const_spec(c.shape) for c in consts],
        out_specs=_seq_spec(FFN_TILE),
        out_shape=jax.ShapeDtypeStruct(x.shape, jnp.float32),
        scratch_shapes=[
            pltpu.VMEM((N_SLABS, FFN_TILE, LANES), jnp.float32),
            pltpu.VMEM((2 * SUBLANES_F32, 2 * D_FF), jnp.float32),
            pltpu.VMEM((FFN_TILE, D_FF), jnp.bfloat16),
            pltpu.VMEM((N_SLABS, FFN_TILE, LANES), jnp.float32),
        ],
        compiler_params=_compiler_params(),
        name="conv_ffn",
    )(x, *consts)


def _rel_rows(rel_table):
    heads = rel_table.shape[0]
    far = rel_table[:, N_REL - 1:]
    near = rel_table[:, :1]
    n_far = HIST - MAX_REL
    return jnp.concatenate([
        jnp.broadcast_to(far, (heads, n_far)),
        rel_table[:, ::-1],
        jnp.broadcast_to(near, (heads, KEY_WIN - n_far - N_REL)),
        jnp.broadcast_to(far, (heads, TOE_LANES - KEY_WIN)),
    ], axis=1)


def _block_diag(w):
    nb, n, _ = w.shape
    eye = jnp.eye(nb, dtype=w.dtype)
    return jnp.einsum("hij,hg->higj", w, eye).reshape(nb * n, nb * n)


def kernel(x, ln_in_g, ln_in_b, w_in, b_in, rel_bias, lru_conv_w, lru_conv_b, lru_w_a, lru_b_a, lru_w_x, lru_b_x, lru_lambda, w_out, b_out, ln1_g, ln1_b, ffn_w_up, ffn_b_up, ffn_conv_w, ffn_conv_b, ffn_w_down, ffn_b_down, ln2_g, ln2_b):
    bf16 = jnp.bfloat16
    row = lambda v: v.reshape(1, -1)
    h = x
    for l in range(DEPTH):
        h = _mixer(h, row(ln_in_g), row(ln_in_b), w_in[l].astype(bf16), row(b_in[l]),
                   _rel_rows(rel_bias[l]), lru_conv_w[l], row(lru_conv_b[l]),
                   _block_diag(lru_w_a[l]).astype(bf16), row(lru_b_a[l]),
                   _block_diag(lru_w_x[l]).astype(bf16), row(lru_b_x[l]), row(lru_lambda[l]),
                   w_out[l].astype(bf16), row(b_out[l]), row(ln1_g[l]), row(ln1_b[l]),
                   pre_ln=(l == 0))
        h = _ffn(h, ffn_w_up[l].astype(bf16), row(ffn_b_up[l]), ffn_conv_w[l],
                 row(ffn_conv_b[l]), ffn_w_down[l].astype(bf16), row(ffn_b_down[l]),
                 row(ln2_g[l]), row(ln2_b[l]))
    return h
```

```python
import functools
import math

import jax
import jax.numpy as jnp
from jax import lax
from jax.experimental import pallas as pl
from jax.experimental.pallas import tpu as pltpu

D_MODEL = 1024
DEPTH = 2
CHUNK = 64
D_ATTN = 512
D_LRU = 512
N_HEADS = 8
HEAD_DIM = 64
N_LRU_BLOCKS = 8
LRU_BLOCK = 64
LRU_CONV_W = 4
LRU_C = 8.0
LEFT_CHUNKS = 8
HIST = LEFT_CHUNKS * CHUNK
MAX_REL = 128
N_REL = (CHUNK - 1) + MAX_REL + 1
D_FF = 2816
FFN_CONV_W = 3
ALPHA = (2 * DEPTH) ** 0.25
LN_EPS = 1e-5
NEG_INF = -1e30

LANES = 128
SUBLANES_F32 = 8

MIX_TILE = 512
Q_BLOCK = 256
KEY_WIN = Q_BLOCK + HIST
TOE_LANES = 1024
assert TOE_LANES >= KEY_WIN + Q_BLOCK - 1
assert HIST <= MIX_TILE
N_BIAS_VARIANTS = 1 + MIX_TILE // Q_BLOCK
HEADS_PER_GROUP = LANES // HEAD_DIM
assert HEADS_PER_GROUP == 2
LRU_SLABS = D_LRU // LANES
BLOCKS_PER_SLAB = LANES // LRU_BLOCK
FFN_TILE = 512
FF_CHUNK = 256
VMEM_LIMIT_BYTES = 56 * 1024 * 1024

_LOG2E = math.log2(math.e)
_GELU_K1 = -2.0 * math.sqrt(2.0 / math.pi) * _LOG2E
_GELU_K3 = _GELU_K1 * 0.044715


def _layer_norm(x, g, b):
    mu = jnp.mean(x, axis=-1, keepdims=True)
    xc = x - mu
    var = jnp.mean(xc * xc, axis=-1, keepdims=True)
    return xc * lax.rsqrt(var + LN_EPS) * g + b


def _gelu_mul(g, v):
    e = jnp.exp2(g * (_GELU_K1 + _GELU_K3 * (g * g)))
    return (g * v) / (1.0 + e)


def _sigmoid(x):
    return 1.0 / (1.0 + jnp.exp2(x * (-_LOG2E)))


def _softplus(x):
    return jnp.maximum(x, 0.0) + jnp.log1p(jnp.exp(-jnp.abs(x)))


def _bdot(a, b):
    return jnp.dot(a, b, preferred_element_type=jnp.float32)


def _interleave_rows(dst, src, rows, width):
    seg = rows // SUBLANES_F32
    for c in range(width // LANES):
        for s in range(SUBLANES_F32):
            dst[c, pl.ds(s, seg, stride=SUBLANES_F32), :] = (
                src[s * seg:(s + 1) * seg, c * LANES:(c + 1) * LANES])


def _deinterleave_slab(dst, src, rows, col0):
    seg = rows // SUBLANES_F32
    for s in range(SUBLANES_F32):
        dst[s * seg:(s + 1) * seg, col0:col0 + LANES] = (
            src[pl.ds(s, seg, stride=SUBLANES_F32), :].astype(dst.dtype))


def _load_slabs(buf):
    return jnp.concatenate([buf[c] for c in range(buf.shape[0])], axis=1)


def _store_slabs(buf, x):
    for c in range(buf.shape[0]):
        buf[c] = x[:, c * LANES:(c + 1) * LANES]


def _time_shifts(x, prev_tail, n):
    rows, sub = x.shape[0], SUBLANES_F32
    first_seg = lax.broadcasted_iota(jnp.int32, (sub, x.shape[1]), 0) == 0
    backs = []
    for i in range(1, n + 1):
        cur = x[rows - i * sub:rows - (i - 1) * sub]
        prev = prev_tail[(n - i) * sub:(n - i + 1) * sub]
        backs.append(jnp.where(first_seg, pltpu.roll(prev, 1, 0), pltpu.roll(cur, 1, 0)))
    return [jnp.concatenate(backs[i - 1::-1] + [x[:rows - i * sub]], axis=0)
            for i in range(1, n + 1)]


def _segment_scan(a, u, h_in):
    rows, sub = a.shape[0], SUBLANES_F32
    prods, locs = [], []
    for j in range(rows // sub):
        aj, uj = a[j * sub:(j + 1) * sub], u[j * sub:(j + 1) * sub]
        if j == 0:
            prod, loc = aj, uj
        else:
            prod, loc = aj * prod, aj * loc + uj
        prods.append(prod)
        locs.append(loc)
    seg_id = lax.broadcasted_iota(jnp.int32, (sub, a.shape[1]), 0)
    enter = jnp.zeros((sub, a.shape[1]), jnp.float32)
    for s in range(sub):
        enter = jnp.where(seg_id == s, h_in, enter)
        h_in = locs[-1][s:s + 1] + prods[-1][s:s + 1] * h_in
    h = jnp.concatenate([loc + prod * enter for loc, prod in zip(locs, prods)], axis=0)
    return h, h_in


def _mixer_kernel(x_ref, lng_ref, lnb_ref, w_in_ref, b_in_ref, rel_ref, cw_ref, cb_ref,
                  wg_ref, bg_ref, lam_ref, w_out_ref, b_out_ref, g_ref, b_ref,
                  o_ref, bias_buf, qe_buf, qo_buf, k_buf, ve_buf, vo_buf, xl_buf, gl_buf,
                  xtail_buf, h_buf, yl_buf, y_buf, *, pre_ln):
    t = pl.program_id(1)
    tile = MIX_TILE
    q_blocks = tile // Q_BLOCK

    @pl.when((pl.program_id(0) == 0) & (t == 0))
    def _():
        qc = lax.broadcasted_iota(jnp.int32, (Q_BLOCK, KEY_WIN), 0) // CHUNK
        kc = lax.broadcasted_iota(jnp.int32, (Q_BLOCK, KEY_WIN), 1) // CHUNK
        in_band = (kc >= qc) & (kc <= qc + LEFT_CHUNKS)
        for h in range(N_HEADS):
            rows = jnp.broadcast_to(rel_ref[h:h + 1, :], (Q_BLOCK, TOE_LANES))
            toe = pltpu.roll(rows, 0, 1, stride=1, stride_axis=0)[:, :KEY_WIN] * _LOG2E
            bias_buf[h] = jnp.where(in_band, toe, NEG_INF)
            for qb in range(q_blocks):
                first_chunk = LEFT_CHUNKS - qb * (Q_BLOCK // CHUNK)
                bias_buf[(1 + qb) * N_HEADS + h] = jnp.where(
                    in_band & (kc >= first_chunk), toe, NEG_INF)

    @pl.when(t == 0)
    def _():
        k_buf[0:HIST, :] = jnp.zeros((HIST, D_ATTN), jnp.bfloat16)
        ve_buf[0:HIST, :] = jnp.zeros((HIST, D_ATTN), jnp.bfloat16)
        vo_buf[0:HIST, :] = jnp.zeros((HIST, D_ATTN), jnp.bfloat16)
        xtail_buf[...] = jnp.zeros(xtail_buf.shape, jnp.float32)
        h_buf[...] = jnp.zeros(h_buf.shape, jnp.float32)

    x = x_ref[0]
    if pre_ln:
        x = _layer_norm(x, lng_ref[...], lnb_ref[...])
    xb = x.astype(jnp.bfloat16)

    def proj(lo, width):
        return _bdot(xb, w_in_ref[:, lo:lo + width]) + b_in_ref[:, lo:lo + width]

    _interleave_rows(xl_buf, proj(3 * D_ATTN, D_LRU), tile, D_LRU)
    _interleave_rows(gl_buf, proj(3 * D_ATTN + D_LRU, D_LRU), tile, D_LRU)

    lane = lax.broadcasted_iota(jnp.int32, (tile, D_ATTN), 1)
    even_head = (lane & (LANES - 1)) < HEAD_DIM
    q = (proj(0, D_ATTN) * (HEAD_DIM ** -0.5 * _LOG2E)).astype(jnp.bfloat16)
    zero = jnp.zeros((tile, D_ATTN), jnp.bfloat16)
    qe_buf[...] = jnp.where(even_head, q, zero)
    qo_buf[...] = jnp.where(even_head, zero, q)
    k_buf[HIST:HIST + tile, :] = proj(D_ATTN, D_ATTN).astype(jnp.bfloat16)
    v = proj(2 * D_ATTN, D_ATTN).astype(jnp.bfloat16)
    one = jnp.ones((tile, D_ATTN), jnp.bfloat16)
    ve_buf[HIST:HIST + tile, :] = jnp.where(even_head, v, one)
    vo_buf[HIST:HIST + tile, :] = jnp.where(even_head, one, v)

    low_half = lax.broadcasted_iota(jnp.int32, (Q_BLOCK, LANES), 1) < HEAD_DIM
    q_bufs, v_bufs = (qe_buf, qo_buf), (ve_buf, vo_buf)

    def scores(qb, grp, par):
        r0, lanes = qb * Q_BLOCK, slice(grp * LANES, (grp + 1) * LANES)
        s = lax.dot_general(q_bufs[par][r0:r0 + Q_BLOCK, lanes], k_buf[r0:r0 + KEY_WIN, lanes],
                            (((1,), (1,)), ((), ())), preferred_element_type=jnp.float32)
        variant = jnp.where(t == 0, (1 + qb) * N_HEADS, 0)
        return s + bias_buf[variant + grp * HEADS_PER_GROUP + par]

    def weighted_values(qb, grp, par, s):
        r0, lanes = qb * Q_BLOCK, slice(grp * LANES, (grp + 1) * LANES)
        p = jnp.exp2(s - jnp.max(s, axis=-1, keepdims=True)).astype(jnp.bfloat16)
        o = _bdot(p, v_bufs[par][r0:r0 + KEY_WIN, lanes])
        return o * pltpu.roll(1.0 / o, HEAD_DIM, 1)

    n_back = LRU_CONV_W - 1

    def lru_slab(c):
        lanes = slice(c * LANES, (c + 1) * LANES)
        xl = xl_buf[c]
        delayed = _time_shifts(xl, xtail_buf[:, lanes], n_back)
        xtail_buf[:, lanes] = xl[tile - n_back * SUBLANES_F32:tile]
        xc = cb_ref[:, lanes] + cw_ref[n_back:n_back + 1, lanes] * xl
        for i in range(1, LRU_CONV_W):
            xc = xc + cw_ref[n_back - i:n_back - i + 1, lanes] * delayed[i - 1]
        gates = _sigmoid(_bdot(xc.astype(jnp.bfloat16), wg_ref[c]) + bg_ref[c])
        r, i_gate = gates[:, :LANES], gates[:, LANES:]
        log_a = r * ((-LRU_C) * _softplus(-lam_ref[:, lanes]))
        tanh_la = jnp.tanh(log_a)
        u = jnp.sqrt(-2.0 * tanh_la / (1.0 - tanh_la)) * (i_gate * xc)
        hs, h_out = _segment_scan(jnp.exp(log_a), u, h_buf[0:1, lanes])
        h_buf[0:1, lanes] = h_out
        yl_buf[c] = _gelu_mul(gl_buf[c], hs)
        _deinterleave_slab(y_buf, yl_buf.at[c], tile, D_ATTN + c * LANES)

    items = [(qb, grp, par) for qb in range(q_blocks)
             for grp in range(N_HEADS // HEADS_PER_GROUP) for par in range(HEADS_PER_GROUP)]
    lru_every = len(items) // LRU_SLABS
    s_next = scores(*items[0])
    for n, (qb, grp, par) in enumerate(items):
        s_cur = s_next
        if n + 1 < len(items):
            s_next = scores(*items[n + 1])
        normed = weighted_values(qb, grp, par, s_cur)
        if par == 0:
            normed_even = normed
        else:
            y_buf[qb * Q_BLOCK:(qb + 1) * Q_BLOCK, grp * LANES:(grp + 1) * LANES] = (
                jnp.where(low_half, normed_even, normed).astype(jnp.bfloat16))
        if n % lru_every == 1:
            lru_slab(n // lru_every)

    k_buf[0:HIST, :] = k_buf[tile:tile + HIST, :]
    ve_buf[0:HIST, :] = ve_buf[tile:tile + HIST, :]
    vo_buf[0:HIST, :] = vo_buf[tile:tile + HIST, :]

    mix = _bdot(y_buf[...], w_out_ref[...]) + b_out_ref[...]
    o_ref[0] = _layer_norm(ALPHA * x + mix, g_ref[...], b_ref[...])


def _ffn_kernel(x_ref, w_up_ref, b_up_ref, fw_ref, fb_ref, w_dn_ref, b_dn_ref, g_ref, b_ref,
                o_ref, xs_buf, tail_buf, gated_buf, ys_buf):
    t = pl.program_id(1)
    tile = FFN_TILE
    n_back = FFN_CONV_W - 1

    @pl.when(t == 0)
    def _():
        tail_buf[...] = jnp.zeros(tail_buf.shape, jnp.float32)

    _interleave_rows(xs_buf, x_ref.at[0], tile, D_MODEL)
    xb = _load_slabs(xs_buf).astype(jnp.bfloat16)

    def conv_chunk(c0):
        cols = slice(c0, c0 + FF_CHUNK)
        u = _bdot(xb, w_up_ref[:, cols]) + b_up_ref[:, cols]
        delayed = _time_shifts(u, tail_buf[:, cols], n_back)
        tail_buf[:, cols] = u[tile - n_back * SUBLANES_F32:tile]
        c = fb_ref[:, cols] + fw_ref[n_back:n_back + 1, cols] * u
        for i in range(1, FFN_CONV_W):
            c = c + fw_ref[n_back - i:n_back - i + 1, cols] * delayed[i - 1]
        return c

    for j in range(D_FF // FF_CHUNK):
        val = conv_chunk(j * FF_CHUNK)
        gate = conv_chunk(D_FF + j * FF_CHUNK)
        gated_buf[:, j * FF_CHUNK:(j + 1) * FF_CHUNK] = _gelu_mul(gate, val).astype(jnp.bfloat16)

    ffn = _bdot(gated_buf[...], w_dn_ref[...]) + b_dn_ref[...]
    _store_slabs(ys_buf, _layer_norm(ALPHA * _load_slabs(xs_buf) + ffn, g_ref[...], b_ref[...]))
    for c in range(D_MODEL // LANES):
        _deinterleave_slab(o_ref.at[0], ys_buf.at[c], tile, c * LANES)


def _const_spec(shape):
    return pl.BlockSpec(shape, lambda b, t: (0,) * len(shape), pipeline_mode=pl.Buffered(1))


def _seq_spec(tile):
    return pl.BlockSpec((1, tile, D_MODEL), lambda b, t: (b, t, 0))


def _compiler_params():
    return pltpu.CompilerParams(dimension_semantics=("arbitrary", "arbitrary"),
                                vmem_limit_bytes=VMEM_LIMIT_BYTES)


def _mixer(x, lng, lnb, w_in, b_in, rel_rows, cw, cb, wg, bg, lam, w_out, b_out, g, b, *, pre_ln):
    bsz, seq, _ = x.shape
    consts = (lng, lnb, w_in, b_in, rel_rows, cw, cb, wg, bg, lam, w_out, b_out, g, b)
    kv_rows = HIST + MIX_TILE
    return pl.pallas_call(
        functools.partial(_mixer_kernel, pre_ln=pre_ln),
        grid=(bsz, seq // MIX_TILE),
        in_specs=[_seq_spec(MIX_TILE)] + [_const_spec(c.shape) for c in consts],
        out_specs=_seq_spec(MIX_TILE),
        out_shape=jax.ShapeDtypeStruct(x.shape, jnp.float32),
        scratch_shapes=[
            pltpu.VMEM((N_BIAS_VARIANTS * N_HEADS, Q_BLOCK, KEY_WIN), jnp.float32),
            pltpu.VMEM((MIX_TILE, D_ATTN), jnp.bfloat16),
            pltpu.VMEM((MIX_TILE, D_ATTN), jnp.bfloat16),
            pltpu.VMEM((kv_rows, D_ATTN), jnp.bfloat16),
            pltpu.VMEM((kv_rows, D_ATTN), jnp.bfloat16),
            pltpu.VMEM((kv_rows, D_ATTN), jnp.bfloat16),
            pltpu.VMEM((LRU_SLABS, MIX_TILE, LANES), jnp.float32),
            pltpu.VMEM((LRU_SLABS, MIX_TILE, LANES), jnp.float32),
            pltpu.VMEM(((LRU_CONV_W - 1) * SUBLANES_F32, D_LRU), jnp.float32),
            pltpu.VMEM((SUBLANES_F32, D_LRU), jnp.float32),
            pltpu.VMEM((LRU_SLABS, MIX_TILE, LANES), jnp.float32),
            pltpu.VMEM((MIX_TILE, D_MODEL), jnp.bfloat16),
        ],
        compiler_params=_compiler_params(),
        name="mixer_pre_ln" if pre_ln else "mixer",
    )(x, *consts)


def _ffn(x, w_up, b_up, fw, fb, w_dn, b_dn, g, b):
    bsz, seq, _ = x.shape
    consts = (w_up, b_up, fw, fb, w_dn, b_dn, g, b)
    slabs = D_MODEL // LANES
    return pl.pallas_call(
        _ffn_kernel,
        grid=(bsz, seq // FFN_TILE),
        in_specs=[_seq_spec(FFN_TILE)] + [_const_spec(c.shape) for c in consts],
        out_specs=_seq_spec(FFN_TILE),
        out_shape=jax.ShapeDtypeStruct(x.shape, jnp.float32),
        scratch_shapes=[
            pltpu.VMEM((slabs, FFN_TILE, LANES), jnp.float32),
            pltpu.VMEM(((FFN_CONV_W - 1) * SUBLANES_F32, 2 * D_FF), jnp.float32),
            pltpu.VMEM((FFN_TILE, D_FF), jnp.bfloat16),
            pltpu.VMEM((slabs, FFN_TILE, LANES), jnp.float32),
        ],
        compiler_params=_compiler_params(),
        name="conv_ffn",
    )(x, *consts)


def _rel_rows(rel_table):
    heads = rel_table.shape[0]
    far = rel_table[:, N_REL - 1:]
    near = rel_table[:, :1]
    n_far = HIST - MAX_REL
    return jnp.concatenate([
        jnp.broadcast_to(far, (heads, n_far)),
        rel_table[:, ::-1],
        jnp.broadcast_to(near, (heads, KEY_WIN - n_far - N_REL)),
        jnp.broadcast_to(far, (heads, TOE_LANES - KEY_WIN)),
    ], axis=1)


def _slab_block_diag(w):
    nb, n, _ = w.shape
    w = w.reshape(nb // BLOCKS_PER_SLAB, BLOCKS_PER_SLAB, n, n)
    eye = jnp.eye(BLOCKS_PER_SLAB, dtype=w.dtype)
    return jnp.einsum("chij,hg->chigj", w, eye).reshape(-1, LANES, LANES)


def _gate_params(w_a, b_a, w_x, b_x):
    wg = jnp.concatenate([_slab_block_diag(w_a), _slab_block_diag(w_x)], axis=2)
    bg = jnp.concatenate([b_a.reshape(LRU_SLABS, 1, LANES), b_x.reshape(LRU_SLABS, 1, LANES)], axis=2)
    return wg.astype(jnp.bfloat16), bg


def kernel(x, ln_in_g, ln_in_b, w_in, b_in, rel_bias, lru_conv_w, lru_conv_b, lru_w_a, lru_b_a, lru_w_x, lru_b_x, lru_lambda, w_out, b_out, ln1_g, ln1_b, ffn_w_up, ffn_b_up, ffn_conv_w, ffn_conv_b, ffn_w_down, ffn_b_down, ln2_g, ln2_b):
    bf16 = jnp.bfloat16
    row = lambda v: v.reshape(1, -1)
    h = x
    for l in range(DEPTH):
        wg, bg = _gate_params(lru_w_a[l], lru_b_a[l], lru_w_x[l], lru_b_x[l])
        h = _mixer(h, row(ln_in_g), row(ln_in_b), w_in[l].astype(bf16), row(b_in[l]),
                   _rel_rows(rel_bias[l]), lru_conv_w[l], row(lru_conv_b[l]), wg, bg,
                   row(lru_lambda[l]), w_out[l].astype(bf16), row(b_out[l]),
                   row(ln1_g[l]), row(ln1_b[l]), pre_ln=(l == 0))
        h = _ffn(h, ffn_w_up[l].astype(bf16), row(ffn_b_up[l]), ffn_conv_w[l],
                 row(ffn_conv_b[l]), ffn_w_down[l].astype(bf16), row(ffn_b_down[l]),
                 row(ln2_g[l]), row(ln2_b[l]))
    return h
```

```python
import functools
import math

import jax
import jax.numpy as jnp
from jax import lax
from jax.experimental import pallas as pl
from jax.experimental.pallas import tpu as pltpu

D_MODEL = 1024
DEPTH = 2
CHUNK = 64
D_ATTN = 512
D_LRU = 512
N_HEADS = 8
HEAD_DIM = 64
N_LRU_BLOCKS = 8
LRU_BLOCK = 64
LRU_CONV_W = 4
LRU_C = 8.0
LEFT_CHUNKS = 8
HIST = LEFT_CHUNKS * CHUNK
MAX_REL = 128
N_REL = (CHUNK - 1) + MAX_REL + 1
D_FF = 2816
FFN_CONV_W = 3
ALPHA = (2 * DEPTH) ** 0.25
LN_EPS = 1e-5
NEG_INF = -1e30

LANES = 128
SUBLANES_F32 = 8

MIX_TILE = 512
Q_BLOCK = 256
KEY_WIN = Q_BLOCK + HIST
TOE_LANES = 1024
assert TOE_LANES >= KEY_WIN + Q_BLOCK - 1
assert HIST <= MIX_TILE
N_BIAS_VARIANTS = 1 + MIX_TILE // Q_BLOCK
HEADS_PER_GROUP = LANES // HEAD_DIM
assert HEADS_PER_GROUP == 2
LRU_SLABS = D_LRU // LANES
BLOCKS_PER_SLAB = LANES // LRU_BLOCK
FFN_TILE = 512
FF_CHUNK = 256
FFN_OUT_ROWS = 256
VMEM_LIMIT_BYTES = 56 * 1024 * 1024

_LOG2E = math.log2(math.e)
_GELU_K1 = -2.0 * math.sqrt(2.0 / math.pi) * _LOG2E
_GELU_K3 = _GELU_K1 * 0.044715


def _layer_norm(x, g, b):
    mu = jnp.mean(x, axis=-1, keepdims=True)
    xc = x - mu
    var = jnp.mean(xc * xc, axis=-1, keepdims=True)
    return xc * lax.rsqrt(var + LN_EPS) * g + b


def _gelu_mul(g, v):
    e = jnp.exp2(g * (_GELU_K1 + _GELU_K3 * (g * g)))
    return (g * v) / (1.0 + e)


def _sigmoid(x):
    return 1.0 / (1.0 + jnp.exp2(x * (-_LOG2E)))


def _softplus(x):
    return jnp.maximum(x, 0.0) + jnp.log1p(jnp.exp(-jnp.abs(x)))


def _bdot(a, b):
    return jnp.dot(a, b, preferred_element_type=jnp.float32)


def _interleave_rows(dst, src, rows, width):
    seg = rows // SUBLANES_F32
    for c in range(width // LANES):
        for s in range(SUBLANES_F32):
            dst[c, pl.ds(s, seg, stride=SUBLANES_F32), :] = (
                src[s * seg:(s + 1) * seg, c * LANES:(c + 1) * LANES])


def _deinterleave_slab(dst, src, rows, col0, part=0, parts=1):
    seg = rows // SUBLANES_F32
    n = seg // parts
    for s in range(SUBLANES_F32):
        dst[s * seg + part * n:s * seg + (part + 1) * n, col0:col0 + LANES] = (
            src[pl.ds(part * n * SUBLANES_F32 + s, n, stride=SUBLANES_F32), :].astype(dst.dtype))


def _load_slabs(buf):
    return jnp.concatenate([buf[c] for c in range(buf.shape[0])], axis=1)


def _time_shifts(x, prev_tail, n):
    rows, sub = x.shape[0], SUBLANES_F32
    first_seg = lax.broadcasted_iota(jnp.int32, (sub, x.shape[1]), 0) == 0
    backs = []
    for i in range(1, n + 1):
        cur = x[rows - i * sub:rows - (i - 1) * sub]
        prev = prev_tail[(n - i) * sub:(n - i + 1) * sub]
        backs.append(jnp.where(first_seg, pltpu.roll(prev, 1, 0), pltpu.roll(cur, 1, 0)))
    return [jnp.concatenate(backs[i - 1::-1] + [x[:rows - i * sub]], axis=0)
            for i in range(1, n + 1)]


def _segment_scan(a, u, h_in):
    rows, sub = a.shape[0], SUBLANES_F32
    prods, locs = [], []
    for j in range(rows // sub):
        aj, uj = a[j * sub:(j + 1) * sub], u[j * sub:(j + 1) * sub]
        if j == 0:
            prod, loc = aj, uj
        else:
            prod, loc = aj * prod, aj * loc + uj
        prods.append(prod)
        locs.append(loc)
    seg_id = lax.broadcasted_iota(jnp.int32, (sub, a.shape[1]), 0)
    enter = jnp.zeros((sub, a.shape[1]), jnp.float32)
    for s in range(sub):
        enter = jnp.where(seg_id == s, h_in, enter)
        h_in = locs[-1][s:s + 1] + prods[-1][s:s + 1] * h_in
    h = jnp.concatenate([loc + prod * enter for loc, prod in zip(locs, prods)], axis=0)
    return h, h_in


def _mixer_kernel(x_ref, lng_ref, lnb_ref, w_in_ref, b_in_ref, rel_ref, cw_ref, cb_ref,
                  wg_ref, bg_ref, lam_ref, w_out_ref, b_out_ref, g_ref, b_ref,
                  o_ref, bias_buf, qe_buf, qo_buf, k_buf, ve_buf, vo_buf, xl_buf, gl_buf,
                  xtail_buf, h_buf, yl_buf, y_buf, *, pre_ln):
    t = pl.program_id(1)
    tile = MIX_TILE
    q_blocks = tile // Q_BLOCK

    @pl.when((pl.program_id(0) == 0) & (t == 0))
    def _():
        qc = lax.broadcasted_iota(jnp.int32, (Q_BLOCK, KEY_WIN), 0) // CHUNK
        kc = lax.broadcasted_iota(jnp.int32, (Q_BLOCK, KEY_WIN), 1) // CHUNK
        in_band = (kc >= qc) & (kc <= qc + LEFT_CHUNKS)
        for h in range(N_HEADS):
            rows = jnp.broadcast_to(rel_ref[h:h + 1, :], (Q_BLOCK, TOE_LANES))
            toe = pltpu.roll(rows, 0, 1, stride=1, stride_axis=0)[:, :KEY_WIN] * _LOG2E
            bias_buf[h] = jnp.where(in_band, toe, NEG_INF)
            for qb in range(q_blocks):
                first_chunk = LEFT_CHUNKS - qb * (Q_BLOCK // CHUNK)
                bias_buf[(1 + qb) * N_HEADS + h] = jnp.where(
                    in_band & (kc >= first_chunk), toe, NEG_INF)

    @pl.when(t == 0)
    def _():
        k_buf[0:HIST, :] = jnp.zeros((HIST, D_ATTN), jnp.bfloat16)
        ve_buf[0:HIST, :] = jnp.zeros((HIST, D_ATTN), jnp.bfloat16)
        vo_buf[0:HIST, :] = jnp.zeros((HIST, D_ATTN), jnp.bfloat16)
        xtail_buf[...] = jnp.zeros(xtail_buf.shape, jnp.float32)
        h_buf[...] = jnp.zeros(h_buf.shape, jnp.float32)

    x = x_ref[0]
    if pre_ln:
        x = _layer_norm(x, lng_ref[...], lnb_ref[...])
    xb = x.astype(jnp.bfloat16)

    def proj(lo, width):
        return _bdot(xb, w_in_ref[:, lo:lo + width]) + b_in_ref[:, lo:lo + width]

    _interleave_rows(xl_buf, proj(3 * D_ATTN, D_LRU), tile, D_LRU)
    _interleave_rows(gl_buf, proj(3 * D_ATTN + D_LRU, D_LRU), tile, D_LRU)

    n_back = LRU_CONV_W - 1

    def lru_slab(c):
        lanes = slice(c * LANES, (c + 1) * LANES)
        xl = xl_buf[c]
        delayed = _time_shifts(xl, xtail_buf[:, lanes], n_back)
        xtail_buf[:, lanes] = xl[tile - n_back * SUBLANES_F32:tile]
        xc = cb_ref[:, lanes] + cw_ref[n_back:n_back + 1, lanes] * xl
        for i in range(1, LRU_CONV_W):
            xc = xc + cw_ref[n_back - i:n_back - i + 1, lanes] * delayed[i - 1]
        gates = _sigmoid(_bdot(xc.astype(jnp.bfloat16), wg_ref[c]) + bg_ref[c])
        r, i_gate = gates[:, :LANES], gates[:, LANES:]
        log_a = r * ((-LRU_C) * _softplus(-lam_ref[:, lanes]))
        tanh_la = jnp.tanh(log_a)
        u = jnp.sqrt(-2.0 * tanh_la / (1.0 - tanh_la)) * (i_gate * xc)
        hs, h_out = _segment_scan(jnp.exp(log_a), u, h_buf[0:1, lanes])
        h_buf[0:1, lanes] = h_out
        yl_buf[c] = _gelu_mul(gl_buf[c], hs)
        _deinterleave_slab(y_buf, yl_buf.at[c], tile, D_ATTN + c * LANES)

    lru_slab(0)

    lane = lax.broadcasted_iota(jnp.int32, (tile, D_ATTN), 1)
    even_head = (lane & (LANES - 1)) < HEAD_DIM
    q = (proj(0, D_ATTN) * (HEAD_DIM ** -0.5 * _LOG2E)).astype(jnp.bfloat16)
    zero = jnp.zeros((tile, D_ATTN), jnp.bfloat16)
    qe_buf[...] = jnp.where(even_head, q, zero)
    qo_buf[...] = jnp.where(even_head, zero, q)
    lru_slab(1)
    k_buf[HIST:HIST + tile, :] = proj(D_ATTN, D_ATTN).astype(jnp.bfloat16)
    lru_slab(2)
    v = proj(2 * D_ATTN, D_ATTN).astype(jnp.bfloat16)
    one = jnp.ones((tile, D_ATTN), jnp.bfloat16)
    ve_buf[HIST:HIST + tile, :] = jnp.where(even_head, v, one)
    vo_buf[HIST:HIST + tile, :] = jnp.where(even_head, one, v)
    lru_slab(3)

    low_half = lax.broadcasted_iota(jnp.int32, (Q_BLOCK, LANES), 1) < HEAD_DIM
    q_bufs, v_bufs = (qe_buf, qo_buf), (ve_buf, vo_buf)

    def scores(qb, grp, par):
        r0, lanes = qb * Q_BLOCK, slice(grp * LANES, (grp + 1) * LANES)
        s = lax.dot_general(q_bufs[par][r0:r0 + Q_BLOCK, lanes], k_buf[r0:r0 + KEY_WIN, lanes],
                            (((1,), (1,)), ((), ())), preferred_element_type=jnp.float32)
        variant = jnp.where(t == 0, (1 + qb) * N_HEADS, 0)
        return s + bias_buf[variant + grp * HEADS_PER_GROUP + par]

    def weighted_values(qb, grp, par, s):
        r0, lanes = qb * Q_BLOCK, slice(grp * LANES, (grp + 1) * LANES)
        p = jnp.exp2(s - jnp.max(s, axis=-1, keepdims=True)).astype(jnp.bfloat16)
        o = _bdot(p, v_bufs[par][r0:r0 + KEY_WIN, lanes])
        return o * pltpu.roll(1.0 / o, HEAD_DIM, 1)

    items = [(qb, grp, par) for qb in range(q_blocks)
             for grp in range(N_HEADS // HEADS_PER_GROUP) for par in range(HEADS_PER_GROUP)]
    s_next = scores(*items[0])
    for n, (qb, grp, par) in enumerate(items):
        s_cur = s_next
        if n + 1 < len(items):
            s_next = scores(*items[n + 1])
        normed = weighted_values(qb, grp, par, s_cur)
        if par == 0:
            normed_even = normed
        else:
            y_buf[qb * Q_BLOCK:(qb + 1) * Q_BLOCK, grp * LANES:(grp + 1) * LANES] = (
                jnp.where(low_half, normed_even, normed).astype(jnp.bfloat16))

    k_buf[0:HIST, :] = k_buf[tile:tile + HIST, :]
    ve_buf[0:HIST, :] = ve_buf[tile:tile + HIST, :]
    vo_buf[0:HIST, :] = vo_buf[tile:tile + HIST, :]

    for r0 in range(0, tile, Q_BLOCK):
        rows = slice(r0, r0 + Q_BLOCK)
        mix = _bdot(y_buf[rows, :], w_out_ref[...]) + b_out_ref[...]
        o_ref[0, rows, :] = _layer_norm(ALPHA * x[rows] + mix, g_ref[...], b_ref[...])


def _ffn_kernel(x_ref, w_up_ref, b_up_ref, fw_ref, fb_ref, w_dn_ref, b_dn_ref, g_ref, b_ref,
                o_ref, xs_buf, tail_buf, gated_buf, ys_buf):
    t = pl.program_id(1)
    tile = FFN_TILE
    n_back = FFN_CONV_W - 1

    @pl.when(t == 0)
    def _():
        tail_buf[...] = jnp.zeros(tail_buf.shape, jnp.float32)

    _interleave_rows(xs_buf, x_ref.at[0], tile, D_MODEL)
    xb = _load_slabs(xs_buf).astype(jnp.bfloat16)

    def conv_chunk(c0):
        cols = slice(c0, c0 + FF_CHUNK)
        u = _bdot(xb, w_up_ref[:, cols]) + b_up_ref[:, cols]
        delayed = _time_shifts(u, tail_buf[:, cols], n_back)
        tail_buf[:, cols] = u[tile - n_back * SUBLANES_F32:tile]
        c = fb_ref[:, cols] + fw_ref[n_back:n_back + 1, cols] * u
        for i in range(1, FFN_CONV_W):
            c = c + fw_ref[n_back - i:n_back - i + 1, cols] * delayed[i - 1]
        return c

    for j in range(D_FF // FF_CHUNK):
        val = conv_chunk(j * FF_CHUNK)
        gate = conv_chunk(D_FF + j * FF_CHUNK)
        gated_buf[:, j * FF_CHUNK:(j + 1) * FF_CHUNK] = _gelu_mul(gate, val).astype(jnp.bfloat16)

    parts = tile // FFN_OUT_ROWS
    for part in range(parts):
        rows = slice(part * FFN_OUT_ROWS, (part + 1) * FFN_OUT_ROWS)
        ffn = _bdot(gated_buf[rows, :], w_dn_ref[...]) + b_dn_ref[...]
        x_rows = jnp.concatenate([xs_buf[c, rows, :] for c in range(D_MODEL // LANES)], axis=1)
        y = _layer_norm(ALPHA * x_rows + ffn, g_ref[...], b_ref[...])
        for c in range(D_MODEL // LANES):
            ys_buf[c, rows, :] = y[:, c * LANES:(c + 1) * LANES]
            _deinterleave_slab(o_ref.at[0], ys_buf.at[c], tile, c * LANES, part, parts)


def _const_spec(arr, layer=None):
    if layer is None:
        shape, index = arr.shape, (0,) * arr.ndim
    else:
        shape, index = (None,) + arr.shape[1:], (layer,) + (0,) * (arr.ndim - 1)
    return pl.BlockSpec(shape, lambda b, t: index, pipeline_mode=pl.Buffered(1))


def _seq_spec(tile):
    return pl.BlockSpec((1, tile, D_MODEL), lambda b, t: (b, t, 0))


def _compiler_params():
    return pltpu.CompilerParams(dimension_semantics=("arbitrary", "arbitrary"),
                                vmem_limit_bytes=VMEM_LIMIT_BYTES)


def _mixer(x, layer, lng, lnb, *stacks, pre_ln):
    bsz, seq, _ = x.shape
    kv_rows = HIST + MIX_TILE
    return pl.pallas_call(
        functools.partial(_mixer_kernel, pre_ln=pre_ln),
        grid=(bsz, seq // MIX_TILE),
        in_specs=([_seq_spec(MIX_TILE), _const_spec(lng), _const_spec(lnb)]
                  + [_const_spec(c, layer) for c in stacks]),
        out_specs=_seq_spec(MIX_TILE),
        out_shape=jax.ShapeDtypeStruct(x.shape, jnp.float32),
        scratch_shapes=[
            pltpu.VMEM((N_BIAS_VARIANTS * N_HEADS, Q_BLOCK, KEY_WIN), jnp.float32),
            pltpu.VMEM((MIX_TILE, D_ATTN), jnp.bfloat16),
            pltpu.VMEM((MIX_TILE, D_ATTN), jnp.bfloat16),
            pltpu.VMEM((kv_rows, D_ATTN), jnp.bfloat16),
            pltpu.VMEM((kv_rows, D_ATTN), jnp.bfloat16),
            pltpu.VMEM((kv_rows, D_ATTN), jnp.bfloat16),
            pltpu.VMEM((LRU_SLABS, MIX_TILE, LANES), jnp.float32),
            pltpu.VMEM((LRU_SLABS, MIX_TILE, LANES), jnp.float32),
            pltpu.VMEM(((LRU_CONV_W - 1) * SUBLANES_F32, D_LRU), jnp.float32),
            pltpu.VMEM((SUBLANES_F32, D_LRU), jnp.float32),
            pltpu.VMEM((LRU_SLABS, MIX_TILE, LANES), jnp.float32),
            pltpu.VMEM((MIX_TILE, D_MODEL), jnp.bfloat16),
        ],
        compiler_params=_compiler_params(),
        name="mixer_pre_ln" if pre_ln else "mixer",
    )(x, lng, lnb, *stacks)


def _ffn(x, layer, *stacks):
    bsz, seq, _ = x.shape
    slabs = D_MODEL // LANES
    return pl.pallas_call(
        _ffn_kernel,
        grid=(bsz, seq // FFN_TILE),
        in_specs=[_seq_spec(FFN_TILE)] + [_const_spec(c, layer) for c in stacks],
        out_specs=_seq_spec(FFN_TILE),
        out_shape=jax.ShapeDtypeStruct(x.shape, jnp.float32),
        scratch_shapes=[
            pltpu.VMEM((slabs, FFN_TILE, LANES), jnp.float32),
            pltpu.VMEM(((FFN_CONV_W - 1) * SUBLANES_F32, 2 * D_FF), jnp.float32),
            pltpu.VMEM((FFN_TILE, D_FF), jnp.bfloat16),
            pltpu.VMEM((slabs, FFN_TILE, LANES), jnp.float32),
        ],
        compiler_params=_compiler_params(),
        name="conv_ffn",
    )(x, *stacks)


def _rel_rows(rel_table):
    heads = rel_table.shape[0]
    far = rel_table[:, N_REL - 1:]
    near = rel_table[:, :1]
    n_far = HIST - MAX_REL
    return jnp.concatenate([
        jnp.broadcast_to(far, (heads, n_far)),
        rel_table[:, ::-1],
        jnp.broadcast_to(near, (heads, KEY_WIN - n_far - N_REL)),
        jnp.broadcast_to(far, (heads, TOE_LANES - KEY_WIN)),
    ], axis=1)


def _slab_block_diag(w):
    nb, n, _ = w.shape
    w = w.reshape(nb // BLOCKS_PER_SLAB, BLOCKS_PER_SLAB, n, n)
    eye = jnp.eye(BLOCKS_PER_SLAB, dtype=w.dtype)
    return jnp.einsum("chij,hg->chigj", w, eye).reshape(-1, LANES, LANES)


def _gate_params(w_a, b_a, w_x, b_x):
    wg = jnp.concatenate([_slab_block_diag(w_a), _slab_block_diag(w_x)], axis=2)
    bg = jnp.concatenate([b_a.reshape(LRU_SLABS, 1, LANES), b_x.reshape(LRU_SLABS, 1, LANES)], axis=2)
    return wg.astype(jnp.bfloat16), bg


def kernel(x, ln_in_g, ln_in_b, w_in, b_in, rel_bias, lru_conv_w, lru_conv_b, lru_w_a, lru_b_a, lru_w_x, lru_b_x, lru_lambda, w_out, b_out, ln1_g, ln1_b, ffn_w_up, ffn_b_up, ffn_conv_w, ffn_conv_b, ffn_w_down, ffn_b_down, ln2_g, ln2_b):
    bf16 = jnp.bfloat16
    rows = lambda v: v.reshape(DEPTH, 1, -1)
    wg, bg = jax.vmap(_gate_params)(lru_w_a, lru_b_a, lru_w_x, lru_b_x)
    mixer_stacks = (w_in.astype(bf16), rows(b_in), jax.vmap(_rel_rows)(rel_bias), lru_conv_w,
                    rows(lru_conv_b), wg, bg, rows(lru_lambda), w_out.astype(bf16), rows(b_out),
                    rows(ln1_g), rows(ln1_b))
    ffn_stacks = (ffn_w_up.astype(bf16), rows(ffn_b_up), ffn_conv_w, rows(ffn_conv_b),
                  ffn_w_down.astype(bf16), rows(ffn_b_down), rows(ln2_g), rows(ln2_b))
    h = x
    for l in range(DEPTH):
        h = _mixer(h, l, ln_in_g.reshape(1, -1), ln_in_b.reshape(1, -1), *mixer_stacks,
                   pre_ln=(l == 0))
        h = _ffn(h, l, *ffn_stacks)
    return h
```
